```python
import math
import jax, jax.numpy as jnp
from jax import lax
import numpy as np

D_MODEL = 2048
BATCH = 4
SEQ = 2048
DEPTH = 4

RMS_EPS = 1e-6
LRU_WIDTH = (3 * D_MODEL) // 4
LRU_BLOCKS = 12
LRU_BLOCK_DIM = LRU_WIDTH // LRU_BLOCKS
LRU_C = 8.0
CONV_WIDTH = 4
ATT_HEAD_DIM = 128
ATT_GROUP_HEADS = 4
ATT_GROUPS = ((128, 1), (512, 4), (2048, 16))
ATT_HEADS = ATT_GROUP_HEADS * len(ATT_GROUPS)
ATT_WIDTH = ATT_HEADS * ATT_HEAD_DIM
ATT_OUT_WIDTH = ATT_GROUP_HEADS * ATT_HEAD_DIM
BLOCK_Q = 128
IN_COLS = 2 * LRU_WIDTH + 3 * ATT_WIDTH + 2 * D_MODEL
DENSE_FF = 3 * D_MODEL
N_EXPERTS = 8
TOP_K = 2
EXPERT_FF = 5632
N_DENSE = (DEPTH + 1) // 2
N_MOE = DEPTH // 2

kernel_name = "hybrid_rglru_dilated_swa_moe_trunk"


def rms_norm(x, g):
    xf = x.astype(jnp.float32)
    y = xf * lax.rsqrt(jnp.mean(xf * xf, axis=-1, keepdims=True) + RMS_EPS)
    return (y * g.astype(jnp.float32)).astype(x.dtype)


def causal_depthwise_conv(x, w, b):
    S = x.shape[1]
    xp = jnp.pad(x, ((0, 0), (CONV_WIDTH - 1, 0), (0, 0)))
    y = b
    for k in range(CONV_WIDTH):
        y = y + w[k] * xp[:, CONV_WIDTH - 1 - k: CONV_WIDTH - 1 - k + S]
    return y


def rg_lru(xb, w_a, b_a, w_x, b_x, lam):
    B, S, W = xb.shape
    xf = xb.astype(jnp.float32)
    xh = xf.reshape(B, S, LRU_BLOCKS, LRU_BLOCK_DIM)
    r = jax.nn.sigmoid(jnp.einsum('bshi,hij->bshj', xh, w_a.astype(jnp.float32)).reshape(B, S, W) + b_a)
    i = jax.nn.sigmoid(jnp.einsum('bshi,hij->bshj', xh, w_x.astype(jnp.float32)).reshape(B, S, W) + b_x)
    log_a = -LRU_C * r * jax.nn.softplus(-lam.astype(jnp.float32))
    a = jnp.exp(log_a)
    mult = jnp.sqrt(-jnp.expm1(2.0 * log_a))
    bterm = mult * (i * xf)

    def combine(left, right):
        a1, b1 = left
        a2, b2 = right
        return (a1 * a2, a2 * b1 + b2)

    _, h = lax.associative_scan(combine, (a, bterm), axis=1)
    return h.astype(xb.dtype)


def qk_rms_norm(t, g):
    tf = t.astype(jnp.float32)
    return tf * lax.rsqrt(jnp.mean(tf * tf, axis=-1, keepdims=True) + RMS_EPS) * g.astype(jnp.float32)


def dilated_group_attention(q, k, v, dilation, steps):
    B, S, H, Dh = q.shape
    L = S // dilation
    bs = math.gcd(L, BLOCK_Q)
    nb = L // bs
    K = bs + steps

    def fold(t):
        return t.reshape(B, L, dilation, H, Dh).transpose(0, 2, 1, 3, 4)

    qb = fold(q).reshape(B, dilation, nb, bs, H, Dh)
    pad = ((0, 0), (0, 0), (steps, 0), (0, 0), (0, 0))
    kp = jnp.pad(fold(k), pad)
    vp = jnp.pad(fold(v), pad)
    idx = (jnp.arange(nb) * bs)[:, None] + jnp.arange(K)[None, :]
    kb = kp[:, :, idx]
    vb = vp[:, :, idx]
    s = jnp.einsum('bdnqhe,bdnkhe->bdnhqk', qb, kb) * (ATT_HEAD_DIM ** -0.5)
    rel = jnp.arange(bs)[:, None] + steps - jnp.arange(K)[None, :]
    valid = ((rel >= 0) & (rel <= steps))[None] & ((idx - steps) >= 0)[:, None, :]
    s = jnp.where(valid[:, None], s, -jnp.inf)
    m = jnp.max(s, axis=-1, keepdims=True)
    p = jnp.exp(s - m)
    den = jnp.sum(p, axis=-1)
    o = jnp.einsum('bdnhqk,bdnkhe->bdnqhe', p, vb)
    den_t = den.transpose(0, 1, 2, 4, 3)
    o = o / den_t[..., None]
    lse = m[..., 0].transpose(0, 1, 2, 4, 3) + jnp.log(den_t)
    o = o.reshape(B, dilation, L, H, Dh).transpose(0, 2, 1, 3, 4).reshape(B, S, H, Dh)
    lse = lse.reshape(B, dilation, L, H).transpose(0, 2, 1, 3).reshape(B, S, H)
    return o, lse


def dilated_attention_branch(q, k, v, q_gain, k_gain):
    B, S = q.shape[:2]
    qn = qk_rms_norm(q, q_gain)
    kn = qk_rms_norm(k, k_gain)
    vf = v.astype(jnp.float32)
    outs, lses = [], []
    for g, (window, dilation) in enumerate(ATT_GROUPS):
        hs = slice(g * ATT_GROUP_HEADS, (g + 1) * ATT_GROUP_HEADS)
        o, lse = dilated_group_attention(qn[:, :, hs], kn[:, :, hs], vf[:, :, hs], dilation, window // dilation)
        outs.append(o)
        lses.append(lse)
    alpha = jax.nn.softmax(jnp.stack(lses, axis=0), axis=0)
    o = jnp.sum(alpha[..., None] * jnp.stack(outs, axis=0), axis=0)
    return o.reshape(B, S, ATT_OUT_WIDTH).astype(q.dtype)


def swiglu(h, w1, w3, w2):
    return (jax.nn.silu(h @ w1) * (h @ w3)) @ w2


def moe_swiglu(h, router_w, router_b, w1, w3, w2):
    logits = (h @ router_w).astype(jnp.float32) + router_b.astype(jnp.float32)
    top_vals, top_idx = lax.top_k(logits, TOP_K)
    top_w = jax.nn.softmax(top_vals, axis=-1)
    gates = jnp.sum(jax.nn.one_hot(top_idx, N_EXPERTS, dtype=jnp.float32) * top_w[..., None], axis=-2)
    gates = gates.astype(h.dtype)
    out = jnp.zeros_like(h)
    for e in range(N_EXPERTS):
        out = out + gates[..., e:e + 1] * swiglu(h, w1[e], w3[e], w2[e])
    return out


def setup_inputs(seed: int = 0) -> dict:
    key = jax.random.key(seed)
    ks = jax.random.split(key, 26)
    f32 = jnp.float32

    def nrm(k, shape, scale):
        return jax.random.normal(k, shape, f32) * scale

    a0 = 0.9 + 0.099 * jax.random.uniform(ks[9], (DEPTH, LRU_WIDTH), f32)
    return {
        "x": nrm(ks[0], (BATCH, SEQ, D_MODEL), 1.0),
        "mix_norm": 1.0 + nrm(ks[1], (DEPTH, D_MODEL), 0.02),
        "w_in": nrm(ks[2], (DEPTH, D_MODEL, IN_COLS), D_MODEL ** -0.5),
        "conv_w": nrm(ks[3], (DEPTH, CONV_WIDTH, LRU_WIDTH), CONV_WIDTH ** -0.5),
        "conv_b": nrm(ks[4], (DEPTH, LRU_WIDTH), 0.01),
        "lru_wa": nrm(ks[5], (DEPTH, LRU_BLOCKS, LRU_BLOCK_DIM, LRU_BLOCK_DIM), LRU_BLOCK_DIM ** -0.5),
        "lru_ba": nrm(ks[6], (DEPTH, LRU_WIDTH), 0.01),
        "lru_wx": nrm(ks[7], (DEPTH, LRU_BLOCKS, LRU_BLOCK_DIM, LRU_BLOCK_DIM), LRU_BLOCK_DIM ** -0.5),
        "lru_bx": nrm(ks[8], (DEPTH, LRU_WIDTH), 0.01),
        "lru_lambda": jnp.log(a0) - jnp.log1p(-a0),
        "q_norm": 1.0 + nrm(ks[10], (DEPTH, ATT_HEAD_DIM), 0.02),
        "k_norm": 1.0 + nrm(ks[11], (DEPTH, ATT_HEAD_DIM), 0.02),
        "w_lru_out": nrm(ks[12], (DEPTH, LRU_WIDTH, D_MODEL), LRU_WIDTH ** -0.5),
        "w_att_out": nrm(ks[13], (DEPTH, ATT_OUT_WIDTH, D_MODEL), ATT_OUT_WIDTH ** -0.5),
        "w_out": nrm(ks[14], (DEPTH, D_MODEL, D_MODEL), D_MODEL ** -0.5),
        "ffn_norm": 1.0 + nrm(ks[15], (DEPTH, D_MODEL), 0.02),
        "dense_w1": nrm(ks[16], (N_DENSE, D_MODEL, DENSE_FF), D_MODEL ** -0.5),
        "dense_w3": nrm(ks[17], (N_DENSE, D_MODEL, DENSE_FF), D_MODEL ** -0.5),
        "dense_w2": nrm(ks[18], (N_DENSE, DENSE_FF, D_MODEL), DENSE_FF ** -0.5),
        "router_w": nrm(ks[19], (N_MOE, D_MODEL, N_EXPERTS), D_MODEL ** -0.5),
        "router_b": nrm(ks[20], (N_MOE, N_EXPERTS), 0.01),
        "moe_w1": nrm(ks[21], (N_MOE, N_EXPERTS, D_MODEL, EXPERT_FF), D_MODEL ** -0.5),
        "moe_w3": nrm(ks[22], (N_MOE, N_EXPERTS, D_MODEL, EXPERT_FF), D_MODEL ** -0.5),
        "moe_w2": nrm(ks[23], (N_MOE, N_EXPERTS, EXPERT_FF, D_MODEL), EXPERT_FF ** -0.5),
    }


def reference(x, mix_norm, w_in, conv_w, conv_b, lru_wa, lru_ba, lru_wx, lru_bx, lru_lambda,
              q_norm, k_norm, w_lru_out, w_att_out, w_out, ffn_norm,
              dense_w1, dense_w3, dense_w2, router_w, router_b, moe_w1, moe_w3, moe_w2):
    B, S, _ = x.shape
    splits = np.cumsum([LRU_WIDTH, LRU_WIDTH, ATT_WIDTH, ATT_WIDTH, ATT_WIDTH, D_MODEL]).tolist()
    for l in range(DEPTH):
        h = rms_norm(x, mix_norm[l])
        proj = h @ w_in[l]
        lru_x, lru_y, q, k, v, g_lru, g_att = jnp.split(proj, splits, axis=-1)
        xb = causal_depthwise_conv(lru_x, conv_w[l], conv_b[l])
        hb = rg_lru(xb, lru_wa[l], lru_ba[l], lru_wx[l], lru_bx[l], lru_lambda[l])
        y_lru = (hb * jax.nn.gelu(lru_y)) @ w_lru_out[l]
        q = q.reshape(B, S, ATT_HEADS, ATT_HEAD_DIM)
        k = k.reshape(B, S, ATT_HEADS, ATT_HEAD_DIM)
        v = v.reshape(B, S, ATT_HEADS, ATT_HEAD_DIM)
        y_att = dilated_attention_branch(q, k, v, q_norm[l], k_norm[l]) @ w_att_out[l]
        merged = jax.nn.sigmoid(g_lru) * y_lru + jax.nn.sigmoid(g_att) * y_att
        x = x + merged @ w_out[l]
        h2 = rms_norm(x, ffn_norm[l])
        j = l // 2
        if l % 2 == 0:
            x = x + swiglu(h2, dense_w1[j], dense_w3[j], dense_w2[j])
        else:
            x = x + moe_swiglu(h2, router_w[j], router_b[j], moe_w1[j], moe_w3[j], moe_w2[j])
    return x
```

```python
import functools

import jax
import jax.numpy as jnp
from jax import lax
from jax.experimental import pallas as pl
from jax.experimental.pallas import tpu as pltpu

F32 = jnp.float32
BF16 = jnp.bfloat16

D_MODEL = 2048
RMS_EPS = 1e-6
LRU_WIDTH = 1536
LRU_BLOCK_DIM = 128
LRU_C = 8.0
CONV_WIDTH = 4
HEAD_DIM = 128
GROUP_HEADS = 4
GROUP_WIDTH = GROUP_HEADS * HEAD_DIM
ATT_DILATIONS = (1, 4, 16)
ATT_STEPS = 128
IN_COLS = 11776
N_EXPERTS = 8
LANES = 128
SUBLANES = 8

CB_LRU_X, CB_LRU_Y, CB_Q, CB_K, CB_V, CB_G_LRU, CB_G_ATT = 0, 3, 6, 9, 12, 15, 19
IN_COL_BLOCKS = IN_COLS // GROUP_WIDTH

NORM_TM = 512
PROJ_TM, PROJ_TN = 1024, 512
LRU_CHUNK = 256
ATT_BQ = 128
MIX_TM, MIX_TN = 512, 512
FFN_TM, FFN_TF = 512, 512
MOE_TM, MOE_TF = 512, 512
ROUTER_TM = 512
COMBINE_TM = 256
DISPATCH_CHUNK = 32
LSE_LANES = LANES // GROUP_HEADS


def _rms(x, gain):
    return x * lax.rsqrt(jnp.mean(x * x, axis=-1, keepdims=True) + RMS_EPS) * gain


def _dot(a, b):
    return jnp.dot(a, b, preferred_element_type=F32)


def _dot_nt(a, b):
    return lax.dot_general(a, b, (((1,), (1,)), ((), ())), preferred_element_type=F32)


def _rmsnorm_kernel(x_ref, g_ref, o_ref):
    o_ref[...] = _rms(x_ref[...], g_ref[...]).astype(o_ref.dtype)


def _rmsnorm(x, gain):
    T, D = x.shape
    return pl.pallas_call(
        _rmsnorm_kernel,
        grid=(T // NORM_TM,),
        in_specs=[pl.BlockSpec((NORM_TM, D), lambda i: (i, 0)),
                  pl.BlockSpec((1, D), lambda i: (0, 0))],
        out_specs=pl.BlockSpec((NORM_TM, D), lambda i: (i, 0)),
        out_shape=jax.ShapeDtypeStruct((T, D), BF16),
        name="rmsnorm",
    )(x, gain.reshape(1, D))


def _matmul_kernel(a_ref, w_ref, o_ref):
    o_ref[...] = _dot(a_ref[...], w_ref[...]).astype(o_ref.dtype)


def _in_proj(h, w):
    T, K = h.shape
    N = w.shape[1]
    return pl.pallas_call(
        _matmul_kernel,
        grid=(T // PROJ_TM, N // PROJ_TN),
        in_specs=[pl.BlockSpec((PROJ_TM, K), lambda i, j: (i, 0)),
                  pl.BlockSpec((K, PROJ_TN), lambda i, j: (0, j))],
        out_specs=pl.BlockSpec((PROJ_TM, PROJ_TN), lambda i, j: (i, j)),
        out_shape=jax.ShapeDtypeStruct((T, N), BF16),
        name="in_proj",
    )(h, w)


def _lru_kernel(x_ref, y_ref, cw_ref, cb_ref, wa_ref, ba_ref, wx_ref, bx_ref, lam_ref, o_ref,
                xpad, a_s, b_s, h_s):
    S, C = x_ref.shape
    xpad[0:SUBLANES, :] = jnp.zeros((SUBLANES, C), F32)
    xpad[SUBLANES:, :] = x_ref[...].astype(F32)
    neg_lam = -lam_ref[...]
    softplus = jnp.maximum(neg_lam, 0.0) + jnp.log(1.0 + jnp.exp(-jnp.abs(neg_lam)))

    def chunk(c, h):
        r0 = pl.multiple_of(c * LRU_CHUNK, LRU_CHUNK)
        xp = xpad[pl.ds(r0, LRU_CHUNK + SUBLANES), :]
        xb = cb_ref[...]
        for k in range(CONV_WIDTH):
            xb = xb + cw_ref[k:k + 1, :] * xp[SUBLANES - k:SUBLANES - k + LRU_CHUNK]
        xb16 = xb.astype(BF16)
        ga, gx = [], []
        for hb in range(C // LRU_BLOCK_DIM):
            sl = slice(hb * LRU_BLOCK_DIM, (hb + 1) * LRU_BLOCK_DIM)
            ga.append(_dot(xb16[:, sl], wa_ref[hb]))
            gx.append(_dot(xb16[:, sl], wx_ref[hb]))
        r = jax.nn.sigmoid(jnp.concatenate(ga, axis=1) + ba_ref[...])
        i = jax.nn.sigmoid(jnp.concatenate(gx, axis=1) + bx_ref[...])
        log_a = -LRU_C * r * softplus
        a_s[...] = jnp.exp(log_a)
        b_s[...] = jnp.sqrt(1.0 - jnp.exp(2.0 * log_a)) * (i * xb)

        def step(t, hc):
            hc = a_s[pl.ds(t, 1), :] * hc + b_s[pl.ds(t, 1), :]
            h_s[pl.ds(t, 1), :] = hc
            return hc

        h = lax.fori_loop(0, LRU_CHUNK, step, h, unroll=8)
        yv = y_ref[pl.ds(r0, LRU_CHUNK), :].astype(F32)
        o_ref[pl.ds(r0, LRU_CHUNK), :] = (h_s[...] * jax.nn.gelu(yv)).astype(o_ref.dtype)
        return h

    lax.fori_loop(0, S // LRU_CHUNK, chunk, jnp.zeros((1, C), F32))


def _lru_branch(proj3, cw, cb, wa, ba, wx, bx, lam):
    B, S, _ = proj3.shape
    C = GROUP_WIDTH
    nblk = C // LRU_BLOCK_DIM
    vec = lambda a: a.reshape(1, LRU_WIDTH)
    vspec = pl.BlockSpec((1, C), lambda b, j: (0, j))
    wspec = pl.BlockSpec((nblk, LRU_BLOCK_DIM, LRU_BLOCK_DIM), lambda b, j: (j, 0, 0))
    return pl.pallas_call(
        _lru_kernel,
        grid=(B, LRU_WIDTH // C),
        in_specs=[pl.BlockSpec((None, S, C), lambda b, j: (b, 0, CB_LRU_X + j)),
                  pl.BlockSpec((None, S, C), lambda b, j: (b, 0, CB_LRU_Y + j)),
                  pl.BlockSpec((CONV_WIDTH, C), lambda b, j: (0, j)),
                  vspec, wspec, vspec, wspec, vspec, vspec],
        out_specs=pl.BlockSpec((None, S, C), lambda b, j: (b, 0, j)),
        out_shape=jax.ShapeDtypeStruct((B, S, LRU_WIDTH), BF16),
        scratch_shapes=[pltpu.VMEM((S + SUBLANES, C), F32),
                        pltpu.VMEM((LRU_CHUNK, C), F32),
                        pltpu.VMEM((LRU_CHUNK, C), F32),
                        pltpu.VMEM((LRU_CHUNK, C), F32)],
        name="rg_lru",
    )(proj3, proj3, cw, vec(cb), wa, vec(ba), wx, vec(bx), vec(lam))


def _att_kernel(*refs, has_prev):
    if has_prev:
        q_ref, kc_ref, vc_ref, kp_ref, vp_ref, qg_ref, kg_ref, o_ref, lse_ref = refs
    else:
        q_ref, kc_ref, vc_ref, qg_ref, kg_ref, o_ref, lse_ref = refs
    bq = q_ref.shape[0]
    row = lax.broadcasted_iota(jnp.int32, (bq, bq), 0)
    col = lax.broadcasted_iota(jnp.int32, (bq, bq), 1)
    scale = HEAD_DIM ** -0.5
    not_first = pl.program_id(2) > 0
    for h in range(GROUP_HEADS):
        sl = slice(h * HEAD_DIM, (h + 1) * HEAD_DIM)
        q = _rms(q_ref[:, sl].astype(F32), qg_ref[...]).astype(BF16)
        kc = _rms(kc_ref[:, sl].astype(F32), kg_ref[...]).astype(BF16)
        s_c = jnp.where(row >= col, _dot_nt(q, kc) * scale, -jnp.inf)
        m = jnp.max(s_c, axis=-1, keepdims=True)
        if has_prev:
            kp = _rms(kp_ref[:, sl].astype(F32), kg_ref[...]).astype(BF16)
            s_p = jnp.where((col >= row) & not_first, _dot_nt(q, kp) * scale, -jnp.inf)
            m = jnp.maximum(m, jnp.max(s_p, axis=-1, keepdims=True))
        p_c = jnp.exp(s_c - m)
        den = jnp.sum(p_c, axis=-1, keepdims=True)
        acc = _dot(p_c.astype(BF16), vc_ref[:, sl])
        if has_prev:
            p_p = jnp.exp(s_p - m)
            den = den + jnp.sum(p_p, axis=-1, keepdims=True)
            acc = acc + _dot(p_p.astype(BF16), vp_ref[:, sl])
        o_ref[:, sl] = acc / den
        lse_ref[:, h * LSE_LANES:(h + 1) * LSE_LANES] = jnp.broadcast_to(m + jnp.log(den), (bq, LSE_LANES))


def _att_group(proj3, g, qg, kg):
    B, S, _ = proj3.shape
    d = ATT_DILATIONS[g]
    L = S // d
    assert ATT_STEPS == ATT_BQ and L % ATT_BQ == 0
    nq = L // ATT_BQ
    has_prev = nq > 1
    pf = proj3.reshape(B, L, d * IN_COLS)

    def cur(cb):
        return pl.BlockSpec((None, ATT_BQ, GROUP_WIDTH), lambda b, r, i: (b, i, r * IN_COL_BLOCKS + cb + g))

    def prev(cb):
        return pl.BlockSpec((None, ATT_BQ, GROUP_WIDTH),
                            lambda b, r, i: (b, jnp.maximum(i - 1, 0), r * IN_COL_BLOCKS + cb + g))

    gspec = pl.BlockSpec((1, HEAD_DIM), lambda b, r, i: (0, 0))
    in_specs = [cur(CB_Q), cur(CB_K), cur(CB_V)]
    args = [pf, pf, pf]
    if has_prev:
        in_specs += [prev(CB_K), prev(CB_V)]
        args += [pf, pf]
    in_specs += [gspec, gspec]
    args += [qg.reshape(1, HEAD_DIM), kg.reshape(1, HEAD_DIM)]
    o, lse = pl.pallas_call(
        functools.partial(_att_kernel, has_prev=has_prev),
        grid=(B, d, nq),
        in_specs=in_specs,
        out_specs=[pl.BlockSpec((None, ATT_BQ, GROUP_WIDTH), lambda b, r, i: (b, i, r)),
                   pl.BlockSpec((None, ATT_BQ, LANES), lambda b, r, i: (b, i, r))],
        out_shape=[jax.ShapeDtypeStruct((B, L, d * GROUP_WIDTH), F32),
                   jax.ShapeDtypeStruct((B, L, d * LANES), F32)],
        name=f"dilated_att_g{g}",
    )(*args)
    return o.reshape(B * S, GROUP_WIDTH), lse.reshape(B * S, LANES)


def _mix_kernel(*refs, moe):
    (al_ref, o0_ref, o1_ref, o2_ref, l0_ref, l1_ref, l2_ref, gl_ref, ga_ref, x_ref,
     wl_ref, wa_ref, wo_ref, nrm_ref) = refs[:14]
    if moe:
        rw_ref, rb_ref, xo_ref, h_ref, lg_ref, att_s, acc_s = refs[14:]
    else:
        xo_ref, h_ref, att_s, acc_s = refs[14:]
    n = pl.program_id(1)

    @pl.when(n == 0)
    def _():
        for h in range(GROUP_HEADS):
            sl = slice(h * HEAD_DIM, (h + 1) * HEAD_DIM)
            ls = [l[:, h * LSE_LANES:h * LSE_LANES + 1] for l in (l0_ref, l1_ref, l2_ref)]
            mx = jnp.maximum(jnp.maximum(ls[0], ls[1]), ls[2])
            ws = [jnp.exp(l - mx) for l in ls]
            o = (ws[0] * o0_ref[:, sl] + ws[1] * o1_ref[:, sl] + ws[2] * o2_ref[:, sl]) / (ws[0] + ws[1] + ws[2])
            att_s[:, sl] = o.astype(att_s.dtype)
        acc_s[...] = jnp.zeros_like(acc_s)

    y_lru = _dot(al_ref[...], wl_ref[...])
    y_att = _dot(att_s[...], wa_ref[...])
    merged = (jax.nn.sigmoid(gl_ref[...].astype(F32)) * y_lru
              + jax.nn.sigmoid(ga_ref[...].astype(F32)) * y_att)
    acc_s[...] += _dot(merged.astype(BF16), wo_ref[...])

    @pl.when(n == pl.num_programs(1) - 1)
    def _():
        xn = x_ref[...] + acc_s[...]
        xo_ref[...] = xn
        hn = _rms(xn, nrm_ref[...])
        h_ref[...] = hn.astype(h_ref.dtype)
        if moe:
            lg_ref[...] = jnp.dot(hn, rw_ref[...], precision=lax.Precision.HIGHEST,
                                  preferred_element_type=F32) + rb_ref[...]


def _mix_out(a_lru, outs, lses, proj, x, w_lru_out, w_att_out, w_out, ffn_gain, router=None):
    T, D = x.shape
    moe = router is not None
    nn = D // MIX_TN
    row = lambda w: pl.BlockSpec((MIX_TM, w), lambda i, n: (i, 0))
    in_specs = [row(LRU_WIDTH)] + [row(GROUP_WIDTH)] * 3 + [row(LANES)] * 3 + [
        pl.BlockSpec((MIX_TM, MIX_TN), lambda i, n: (i, CB_G_LRU + n)),
        pl.BlockSpec((MIX_TM, MIX_TN), lambda i, n: (i, CB_G_ATT + n)),
        row(D),
        pl.BlockSpec((LRU_WIDTH, MIX_TN), lambda i, n: (0, n)),
        pl.BlockSpec((GROUP_WIDTH, MIX_TN), lambda i, n: (0, n)),
        pl.BlockSpec((MIX_TN, D), lambda i, n: (n, 0)),
        pl.BlockSpec((1, D), lambda i, n: (0, 0)),
    ]
    args = [a_lru, *outs, *lses, proj, proj, x, w_lru_out, w_att_out, w_out, ffn_gain.reshape(1, D)]
    out_specs = [row(D), row(D)]
    out_shape = [jax.ShapeDtypeStruct((T, D), F32), jax.ShapeDtypeStruct((T, D), F32 if moe else BF16)]
    if moe:
        rw, rb = router
        in_specs += [pl.BlockSpec((D, LANES), lambda i, n: (0, 0)), pl.BlockSpec((1, LANES), lambda i, n: (0, 0))]
        args += [rw, rb]
        out_specs.append(row(LANES))
        out_shape.append(jax.ShapeDtypeStruct((T, LANES), F32))
    return pl.pallas_call(
        functools.partial(_mix_kernel, moe=moe),
        grid=(T // MIX_TM, nn),
        in_specs=in_specs,
        out_specs=out_specs,
        out_shape=out_shape,
        scratch_shapes=[pltpu.VMEM((MIX_TM, GROUP_WIDTH), BF16), pltpu.VMEM((MIX_TM, D), F32)],
        compiler_params=pltpu.CompilerParams(dimension_semantics=("parallel", "arbitrary")),
        name="mix_out_moe" if moe else "mix_out",
    )(*args)


def _swiglu_partial(xb, w1, w3, w2):
    g = (jax.nn.silu(_dot(xb, w1)) * _dot(xb, w3)).astype(BF16)
    return _dot(g, w2)


def _ffn_kernel(*refs, has_next):
    if has_next:
        h_ref, x_ref, w1_ref, w3_ref, w2_ref, g_ref, xo_ref, hn_ref, acc_s = refs
    else:
        h_ref, x_ref, w1_ref, w3_ref, w2_ref, xo_ref, acc_s = refs
    f = pl.program_id(1)
    part = _swiglu_partial(h_ref[...], w1_ref[...], w3_ref[...], w2_ref[...])

    @pl.when(f == 0)
    def _():
        acc_s[...] = x_ref[...] + part

    @pl.when(f > 0)
    def _():
        acc_s[...] += part

    @pl.when(f == pl.num_programs(1) - 1)
    def _():
        xn = acc_s[...]
        xo_ref[...] = xn
        if has_next:
            hn_ref[...] = _rms(xn, g_ref[...]).astype(hn_ref.dtype)


def _dense_ffn(h, x, w1, w3, w2, next_gain):
    T, D = x.shape
    FF = w1.shape[1]
    has_next = next_gain is not None
    row = pl.BlockSpec((FFN_TM, D), lambda i, f: (i, 0))
    in_specs = [row, row,
                pl.BlockSpec((D, FFN_TF), lambda i, f: (0, f)),
                pl.BlockSpec((D, FFN_TF), lambda i, f: (0, f)),
                pl.BlockSpec((FFN_TF, D), lambda i, f: (f, 0))]
    args = [h, x, w1, w3, w2]
    out_specs, out_shape = [row], [jax.ShapeDtypeStruct((T, D), F32)]
    if has_next:
        in_specs.append(pl.BlockSpec((1, D), lambda i, f: (0, 0)))
        args.append(next_gain.reshape(1, D))
        out_specs.append(row)
        out_shape.append(jax.ShapeDtypeStruct((T, D), BF16))
    res = pl.pallas_call(
        functools.partial(_ffn_kernel, has_next=has_next),
        grid=(T // FFN_TM, FF // FFN_TF),
        in_specs=in_specs, out_specs=out_specs, out_shape=out_shape,
        scratch_shapes=[pltpu.VMEM((FFN_TM, D), F32)],
        compiler_params=pltpu.CompilerParams(dimension_semantics=("parallel", "arbitrary")),
        name="dense_ffn",
    )(*args)
    return (res[0], res[1]) if has_next else (res[0], None)


def _router_kernel(lg_ref, idx_ref, wt_ref, cnt_ref, carry_s):
    tm = lg_ref.shape[0]

    @pl.when(pl.program_id(0) == 0)
    def _():
        carry_s[...] = jnp.zeros_like(carry_s)

    lane = lax.broadcasted_iota(jnp.int32, (tm, LANES), 1)
    lg = jnp.where(lane < N_EXPERTS, lg_ref[...], -jnp.inf)
    m1 = jnp.max(lg, axis=-1, keepdims=True)
    i1 = jnp.min(jnp.where(lg == m1, lane, LANES), axis=-1, keepdims=True)
    lg2 = jnp.where(lane == i1, -jnp.inf, lg)
    m2 = jnp.max(lg2, axis=-1, keepdims=True)
    i2 = jnp.min(jnp.where(lg2 == m2, lane, LANES), axis=-1, keepdims=True)
    e = jnp.exp(m2 - m1)
    w1 = 1.0 / (1.0 + e)
    w2 = e / (1.0 + e)
    hit = (lane == i1) | (lane == i2)
    rr = lax.broadcasted_iota(jnp.int32, (tm, tm), 0)
    cc = lax.broadcasted_iota(jnp.int32, (tm, tm), 1)
    excl = _dot((rr > cc).astype(BF16), hit.astype(BF16)) + carry_s[...]
    r1 = jnp.sum(jnp.where(lane == i1, excl, 0.0), axis=-1, keepdims=True).astype(jnp.int32)
    r2 = jnp.sum(jnp.where(lane == i2, excl, 0.0), axis=-1, keepdims=True).astype(jnp.int32)
    idx_ref[...] = jnp.where(lane == 0, i1, jnp.where(lane == 1, i2, jnp.where(lane == 2, r1, jnp.where(lane == 3, r2, 0))))
    wt_ref[...] = jnp.where(lane == 0, w1, jnp.where(lane == 1, w2, 0.0))
    carry_s[...] += jnp.sum(hit.astype(F32), axis=0, keepdims=True)
    cnt_ref[...] = carry_s[...].astype(jnp.int32)


def _router(logits):
    T = logits.shape[0]
    row = pl.BlockSpec((ROUTER_TM, LANES), lambda i: (i, 0))
    return pl.pallas_call(
        _router_kernel,
        grid=(T // ROUTER_TM,),
        in_specs=[row],
        out_specs=[row, row, pl.BlockSpec((1, LANES), lambda i: (0, 0))],
        out_shape=[jax.ShapeDtypeStruct((T, LANES), jnp.int32),
                   jax.ShapeDtypeStruct((T, LANES), F32),
                   jax.ShapeDtypeStruct((1, LANES), jnp.int32)],
        scratch_shapes=[pltpu.VMEM((1, LANES), F32)],
        compiler_params=pltpu.CompilerParams(dimension_semantics=("arbitrary",)),
        name="moe_router",
    )(logits)


def _row_copy(src, s, dst, d, sem):
    return pltpu.make_async_copy(src.at[pl.ds(s, 1)], dst.at[pl.ds(d, 1)], sem)


def _dispatch_kernel(pos1_ref, pos2_ref, h_hbm, xs_in_hbm, xs_hbm, sem):
    del xs_in_hbm
    T = h_hbm.shape[0]
    nchunk = T // DISPATCH_CHUNK

    def issue(c):
        for j in range(DISPATCH_CHUNK):
            t = c * DISPATCH_CHUNK + j
            _row_copy(h_hbm, t, xs_hbm, pos1_ref[t], sem).start()
            _row_copy(h_hbm, t, xs_hbm, pos2_ref[t], sem).start()

    def drain():
        for _ in range(2 * DISPATCH_CHUNK):
            _row_copy(h_hbm, 0, xs_hbm, 0, sem).wait()

    issue(0)

    def body(c, carry):
        issue(c)
        drain()
        return carry

    lax.fori_loop(1, nchunk, body, 0)
    drain()


def _dispatch(pos1, pos2, h, n_rows):
    T, D = h.shape
    return pl.pallas_call(
        _dispatch_kernel,
        grid_spec=pltpu.PrefetchScalarGridSpec(
            num_scalar_prefetch=2,
            grid=(1,),
            in_specs=[pl.BlockSpec(memory_space=pl.ANY), pl.BlockSpec(memory_space=pl.ANY)],
            out_specs=pl.BlockSpec(memory_space=pl.ANY),
            scratch_shapes=[pltpu.SemaphoreType.DMA(())],
        ),
        out_shape=jax.ShapeDtypeStruct((n_rows, D), F32),
        input_output_aliases={3: 0},
        name="moe_dispatch",
    )(pos1, pos2, h, jnp.zeros((n_rows, D), F32))


def _expert_kernel(te_ref, na_ref, xs_ref, w1_ref, w3_ref, w2_ref, ys_ref, xb_s, acc_s):
    del te_ref
    i = pl.program_id(0)
    f = pl.program_id(1)

    @pl.when(i < na_ref[0])
    def _():
        @pl.when(f == 0)
        def _():
            xb_s[...] = xs_ref[...].astype(BF16)

        part = _swiglu_partial(xb_s[...], w1_ref[0], w3_ref[0], w2_ref[0])

        @pl.when(f == 0)
        def _():
            acc_s[...] = part

        @pl.when(f > 0)
        def _():
            acc_s[...] += part

        @pl.when(f == pl.num_programs(1) - 1)
        def _():
            ys_ref[...] = acc_s[...]


def _experts(tile_expert, n_active, xs, w1, w3, w2):
    P, D = xs.shape
    FF = w1.shape[2]
    nf = FF // MOE_TF

    def tile(i, na):
        return jnp.minimum(i, na[0] - 1)

    def fidx(i, f, na):
        return jnp.where(i < na[0], f, nf - 1)

    row = pl.BlockSpec((MOE_TM, D), lambda i, f, te, na: (tile(i, na), 0))
    return pl.pallas_call(
        _expert_kernel,
        grid_spec=pltpu.PrefetchScalarGridSpec(
            num_scalar_prefetch=2,
            grid=(P // MOE_TM, nf),
            in_specs=[row,
                      pl.BlockSpec((1, D, MOE_TF), lambda i, f, te, na: (te[tile(i, na)], 0, fidx(i, f, na))),
                      pl.BlockSpec((1, D, MOE_TF), lambda i, f, te, na: (te[tile(i, na)], 0, fidx(i, f, na))),
                      pl.BlockSpec((1, MOE_TF, D), lambda i, f, te, na: (te[tile(i, na)], fidx(i, f, na), 0))],
            out_specs=row,
            scratch_shapes=[pltpu.VMEM((MOE_TM, D), BF16), pltpu.VMEM((MOE_TM, D), F32)],
        ),
        out_shape=jax.ShapeDtypeStruct((P, D), F32),
        compiler_params=pltpu.CompilerParams(dimension_semantics=("arbitrary", "arbitrary")),
        name="moe_experts",
    )(tile_expert, n_active, xs, w1, w3, w2)


def _combine_kernel(*refs, has_next):
    if has_next:
        pos1_ref, pos2_ref, ys_hbm, x_ref, wt_ref, g_ref, xo_ref, hn_ref, b1, b2, sem = refs
    else:
        pos1_ref, pos2_ref, ys_hbm, x_ref, wt_ref, xo_ref, b1, b2, sem = refs
    tm = x_ref.shape[0]
    base = pl.program_id(0) * tm

    def issue(j, carry):
        t = base + j
        _row_copy(ys_hbm, pos1_ref[t], b1, j, sem).start()
        _row_copy(ys_hbm, pos2_ref[t], b2, j, sem).start()
        return carry

    lax.fori_loop(0, tm, issue, 0, unroll=8)

    def drain(j, carry):
        _row_copy(ys_hbm, 0, b1, 0, sem).wait()
        _row_copy(ys_hbm, 0, b2, 0, sem).wait()
        return carry

    lax.fori_loop(0, tm, drain, 0, unroll=8)
    xn = x_ref[...] + (wt_ref[:, 0:1] * b1[...] + wt_ref[:, 1:2] * b2[...])
    xo_ref[...] = xn
    if has_next:
        hn_ref[...] = _rms(xn, g_ref[...]).astype(hn_ref.dtype)


def _combine(pos1, pos2, ys, x, wts, next_gain):
    T, D = x.shape
    has_next = next_gain is not None
    row = pl.BlockSpec((COMBINE_TM, D), lambda i, p1, p2: (i, 0))
    in_specs = [pl.BlockSpec(memory_space=pl.ANY), row,
                pl.BlockSpec((COMBINE_TM, LANES), lambda i, p1, p2: (i, 0))]
    args = [ys, x, wts]
    out_specs, out_shape = [row], [jax.ShapeDtypeStruct((T, D), F32)]
    if has_next:
        in_specs.append(pl.BlockSpec((1, D), lambda i, p1, p2: (0, 0)))
        args.append(next_gain.reshape(1, D))
        out_specs.append(row)
        out_shape.append(jax.ShapeDtypeStruct((T, D), BF16))
    res = pl.pallas_call(
        functools.partial(_combine_kernel, has_next=has_next),
        grid_spec=pltpu.PrefetchScalarGridSpec(
            num_scalar_prefetch=2,
            grid=(T // COMBINE_TM,),
            in_specs=in_specs, out_specs=out_specs,
            scratch_shapes=[pltpu.VMEM((COMBINE_TM, D), F32), pltpu.VMEM((COMBINE_TM, D), F32),
                            pltpu.SemaphoreType.DMA(())],
        ),
        out_shape=out_shape,
        compiler_params=pltpu.CompilerParams(dimension_semantics=("arbitrary",)),
        name="moe_combine",
    )(pos1, pos2, *args)
    return (res[0], res[1]) if has_next else (res[0], None)


def _moe_ffn(h, logits, x, w1, w3, w2, next_gain):
    T, D = x.shape
    idx, wts, cnt = _router(logits)
    e1, e2, r1, r2 = idx[:, 0], idx[:, 1], idx[:, 2], idx[:, 3]
    counts = cnt[0, :N_EXPERTS]
    padded = (counts + MOE_TM - 1) // MOE_TM * MOE_TM
    ends = jnp.cumsum(padded)
    starts = ends - padded
    experts = jnp.arange(N_EXPERTS, dtype=jnp.int32)
    pos1 = jnp.sum(jnp.where(e1[:, None] == experts, starts, 0), axis=1).astype(jnp.int32) + r1
    pos2 = jnp.sum(jnp.where(e2[:, None] == experts, starts, 0), axis=1).astype(jnp.int32) + r2
    n_tiles = 2 * T // MOE_TM + N_EXPERTS
    n_active = (ends[-1] // MOE_TM).astype(jnp.int32).reshape(1)
    tile_rows = jnp.arange(n_tiles, dtype=jnp.int32) * MOE_TM
    tile_expert = jnp.minimum(jnp.sum(tile_rows[:, None] >= ends[None, :], axis=1), N_EXPERTS - 1).astype(jnp.int32)
    xs = _dispatch(pos1, pos2, h, n_tiles * MOE_TM)
    ys = _experts(tile_expert, n_active, xs, w1, w3, w2)
    return _combine(pos1, pos2, ys, x, wts, next_gain)


def kernel(x, mix_norm, w_in, conv_w, conv_b, lru_wa, lru_ba, lru_wx, lru_bx, lru_lambda, q_norm, k_norm,
           w_lru_out, w_att_out, w_out, ffn_norm, dense_w1, dense_w3, dense_w2, router_w, router_b,
           moe_w1, moe_w3, moe_w2):
    B, S, D = x.shape
    T = B * S
    depth = w_in.shape[0]
    cast = lambda w: w.astype(BF16)
    xf = x.reshape(T, D)
    h = _rmsnorm(xf, mix_norm[0])
    for l in range(depth):
        proj = _in_proj(h, cast(w_in[l]))
        proj3 = proj.reshape(B, S, IN_COLS)
        a_lru = _lru_branch(proj3, conv_w[l], conv_b[l], cast(lru_wa[l]), lru_ba[l], cast(lru_wx[l]),
                            lru_bx[l], lru_lambda[l]).reshape(T, LRU_WIDTH)
        groups = [_att_group(proj3, g, q_norm[l], k_norm[l]) for g in range(len(ATT_DILATIONS))]
        outs = [o for o, _ in groups]
        lses = [s for _, s in groups]
        next_gain = mix_norm[l + 1] if l + 1 < depth else None
        j = l // 2
        if l % 2 == 0:
            xf, h2 = _mix_out(a_lru, outs, lses, proj, xf, cast(w_lru_out[l]), cast(w_att_out[l]),
                              cast(w_out[l]), ffn_norm[l])
            xf, h = _dense_ffn(h2, xf, cast(dense_w1[j]), cast(dense_w3[j]), cast(dense_w2[j]), next_gain)
        else:
            rw = jnp.pad(router_w[j], ((0, 0), (0, LANES - N_EXPERTS)))
            rb = jnp.pad(router_b[j], (0, LANES - N_EXPERTS)).reshape(1, LANES)
            xf, h2, logits = _mix_out(a_lru, outs, lses, proj, xf, cast(w_lru_out[l]), cast(w_att_out[l]),
                                      cast(w_out[l]), ffn_norm[l], router=(rw, rb))
            xf, h = _moe_ffn(h2, logits, xf, cast(moe_w1[j]), cast(moe_w3[j]), cast(moe_w2[j]), next_gain)
    return xf.reshape(B, S, D)
```

```python
import functools

import jax
import jax.numpy as jnp
from jax import lax
from jax.experimental import pallas as pl
from jax.experimental.pallas import tpu as pltpu

F32 = jnp.float32
BF16 = jnp.bfloat16

D_MODEL = 2048
RMS_EPS = 1e-6
LRU_WIDTH = 1536
LRU_BLOCK_DIM = 128
LRU_C = 8.0
CONV_WIDTH = 4
HEAD_DIM = 128
GROUP_HEADS = 4
GROUP_WIDTH = GROUP_HEADS * HEAD_DIM
ATT_DILATIONS = (1, 4, 16)
ATT_STEPS = 128
IN_COLS = 11776
N_EXPERTS = 8
LANES = 128
SUBLANES = 8

CB_LRU_X, CB_LRU_Y, CB_Q, CB_K, CB_V, CB_G_LRU, CB_G_ATT = 0, 3, 6, 9, 12, 15, 19
IN_COL_BLOCKS = IN_COLS // GROUP_WIDTH

NORM_TM = 512
PROJ_TM, PROJ_TN = 1024, 512
LRU_CHUNK = 256
ATT_BQ = 128
MIX_TM, MIX_TN = 512, 512
FFN_TM, FFN_TF = 512, 512
MOE_TM, MOE_TF = 512, 512
ROUTER_TM = 512
COMBINE_TM = 256
DISPATCH_TM = 256
LSE_LANES = LANES // GROUP_HEADS


def _rms(x, gain):
    return x * lax.rsqrt(jnp.mean(x * x, axis=-1, keepdims=True) + RMS_EPS) * gain


def _dot(a, b):
    return jnp.dot(a, b, preferred_element_type=F32)


def _dot_nt(a, b):
    return lax.dot_general(a, b, (((1,), (1,)), ((), ())), preferred_element_type=F32)


def _rmsnorm_kernel(x_ref, g_ref, o_ref):
    o_ref[...] = _rms(x_ref[...], g_ref[...]).astype(o_ref.dtype)


def _rmsnorm(x, gain):
    T, D = x.shape
    return pl.pallas_call(
        _rmsnorm_kernel,
        grid=(T // NORM_TM,),
        in_specs=[pl.BlockSpec((NORM_TM, D), lambda i: (i, 0)),
                  pl.BlockSpec((1, D), lambda i: (0, 0))],
        out_specs=pl.BlockSpec((NORM_TM, D), lambda i: (i, 0)),
        out_shape=jax.ShapeDtypeStruct((T, D), BF16),
        name="rmsnorm",
    )(x, gain.reshape(1, D))


def _matmul_kernel(a_ref, w_ref, o_ref):
    o_ref[...] = _dot(a_ref[...], w_ref[...]).astype(o_ref.dtype)


def _in_proj(h, w):
    T, K = h.shape
    N = w.shape[1]
    return pl.pallas_call(
        _matmul_kernel,
        grid=(T // PROJ_TM, N // PROJ_TN),
        in_specs=[pl.BlockSpec((PROJ_TM, K), lambda i, j: (i, 0)),
                  pl.BlockSpec((K, PROJ_TN), lambda i, j: (0, j))],
        out_specs=pl.BlockSpec((PROJ_TM, PROJ_TN), lambda i, j: (i, j)),
        out_shape=jax.ShapeDtypeStruct((T, N), BF16),
        name="in_proj",
    )(h, w)


def _lru_kernel(x_ref, y_ref, cw_ref, cb_ref, wa_ref, ba_ref, wx_ref, bx_ref, lam_ref, o_ref,
                xpad, a_s, b_s, h_s):
    S, C = x_ref.shape
    xpad[0:SUBLANES, :] = jnp.zeros((SUBLANES, C), F32)
    xpad[SUBLANES:, :] = x_ref[...].astype(F32)
    neg_lam = -lam_ref[...]
    softplus = jnp.maximum(neg_lam, 0.0) + jnp.log(1.0 + jnp.exp(-jnp.abs(neg_lam)))

    def chunk(c, h):
        r0 = pl.multiple_of(c * LRU_CHUNK, LRU_CHUNK)
        xp = xpad[pl.ds(r0, LRU_CHUNK + SUBLANES), :]
        xb = cb_ref[...]
        for k in range(CONV_WIDTH):
            xb = xb + cw_ref[k:k + 1, :] * xp[SUBLANES - k:SUBLANES - k + LRU_CHUNK]
        xb16 = xb.astype(BF16)
        ga, gx = [], []
        for hb in range(C // LRU_BLOCK_DIM):
            sl = slice(hb * LRU_BLOCK_DIM, (hb + 1) * LRU_BLOCK_DIM)
            ga.append(_dot(xb16[:, sl], wa_ref[hb]))
            gx.append(_dot(xb16[:, sl], wx_ref[hb]))
        r = jax.nn.sigmoid(jnp.concatenate(ga, axis=1) + ba_ref[...])
        i = jax.nn.sigmoid(jnp.concatenate(gx, axis=1) + bx_ref[...])
        log_a = -LRU_C * r * softplus
        a_s[...] = jnp.exp(log_a)
        b_s[...] = jnp.sqrt(1.0 - jnp.exp(2.0 * log_a)) * (i * xb)

        def step(t, hc):
            hc = a_s[pl.ds(t, 1), :] * hc + b_s[pl.ds(t, 1), :]
            h_s[pl.ds(t, 1), :] = hc
            return hc

        h = lax.fori_loop(0, LRU_CHUNK, step, h, unroll=8)
        yv = y_ref[pl.ds(r0, LRU_CHUNK), :].astype(F32)
        o_ref[pl.ds(r0, LRU_CHUNK), :] = (h_s[...] * jax.nn.gelu(yv)).astype(o_ref.dtype)
        return h

    lax.fori_loop(0, S // LRU_CHUNK, chunk, jnp.zeros((1, C), F32))


def _lru_branch(proj3, cw, cb, wa, ba, wx, bx, lam):
    B, S, _ = proj3.shape
    C = GROUP_WIDTH
    nblk = C // LRU_BLOCK_DIM
    vec = lambda a: a.reshape(1, LRU_WIDTH)
    vspec = pl.BlockSpec((1, C), lambda b, j: (0, j))
    wspec = pl.BlockSpec((nblk, LRU_BLOCK_DIM, LRU_BLOCK_DIM), lambda b, j: (j, 0, 0))
    return pl.pallas_call(
        _lru_kernel,
        grid=(B, LRU_WIDTH // C),
        in_specs=[pl.BlockSpec((None, S, C), lambda b, j: (b, 0, CB_LRU_X + j)),
                  pl.BlockSpec((None, S, C), lambda b, j: (b, 0, CB_LRU_Y + j)),
                  pl.BlockSpec((CONV_WIDTH, C), lambda b, j: (0, j)),
                  vspec, wspec, vspec, wspec, vspec, vspec],
        out_specs=pl.BlockSpec((None, S, C), lambda b, j: (b, 0, j)),
        out_shape=jax.ShapeDtypeStruct((B, S, LRU_WIDTH), BF16),
        scratch_shapes=[pltpu.VMEM((S + SUBLANES, C), F32),
                        pltpu.VMEM((LRU_CHUNK, C), F32),
                        pltpu.VMEM((LRU_CHUNK, C), F32),
                        pltpu.VMEM((LRU_CHUNK, C), F32)],
        name="rg_lru",
    )(proj3, proj3, cw, vec(cb), wa, vec(ba), wx, vec(bx), vec(lam))


def _att_kernel(*refs, has_prev):
    if has_prev:
        q_ref, kc_ref, vc_ref, kp_ref, vp_ref, qg_ref, kg_ref, o_ref, lse_ref = refs
    else:
        q_ref, kc_ref, vc_ref, qg_ref, kg_ref, o_ref, lse_ref = refs
    bq = q_ref.shape[0]
    row = lax.broadcasted_iota(jnp.int32, (bq, bq), 0)
    col = lax.broadcasted_iota(jnp.int32, (bq, bq), 1)
    scale = HEAD_DIM ** -0.5
    not_first = pl.program_id(2) > 0
    for h in range(GROUP_HEADS):
        sl = slice(h * HEAD_DIM, (h + 1) * HEAD_DIM)
        q = _rms(q_ref[:, sl].astype(F32), qg_ref[...]).astype(BF16)
        kc = _rms(kc_ref[:, sl].astype(F32), kg_ref[...]).astype(BF16)
        s_c = jnp.where(row >= col, _dot_nt(q, kc) * scale, -jnp.inf)
        m = jnp.max(s_c, axis=-1, keepdims=True)
        if has_prev:
            kp = _rms(kp_ref[:, sl].astype(F32), kg_ref[...]).astype(BF16)
            s_p = jnp.where((col >= row) & not_first, _dot_nt(q, kp) * scale, -jnp.inf)
            m = jnp.maximum(m, jnp.max(s_p, axis=-1, keepdims=True))
        p_c = jnp.exp(s_c - m)
        den = jnp.sum(p_c, axis=-1, keepdims=True)
        acc = _dot(p_c.astype(BF16), vc_ref[:, sl])
        if has_prev:
            p_p = jnp.exp(s_p - m)
            den = den + jnp.sum(p_p, axis=-1, keepdims=True)
            acc = acc + _dot(p_p.astype(BF16), vp_ref[:, sl])
        o_ref[:, sl] = acc / den
        lse_ref[:, h * LSE_LANES:(h + 1) * LSE_LANES] = jnp.broadcast_to(m + jnp.log(den), (bq, LSE_LANES))


def _att_group(proj3, g, qg, kg):
    B, S, _ = proj3.shape
    d = ATT_DILATIONS[g]
    L = S // d
    assert ATT_STEPS == ATT_BQ and L % ATT_BQ == 0
    nq = L // ATT_BQ
    has_prev = nq > 1
    if d == 1:
        pf, ncb, cbs = proj3, IN_COL_BLOCKS, (CB_Q + g, CB_K + g, CB_V + g)
    else:
        cols = [proj3[:, :, (cb + g) * GROUP_WIDTH:(cb + g + 1) * GROUP_WIDTH] for cb in (CB_Q, CB_K, CB_V)]
        ncb, cbs = len(cols), (0, 1, 2)
        pf = jnp.concatenate(cols, axis=-1).reshape(B, L, d * ncb * GROUP_WIDTH)

    def cur(cb):
        return pl.BlockSpec((None, ATT_BQ, GROUP_WIDTH), lambda b, r, i: (b, i, r * ncb + cb))

    def prev(cb):
        return pl.BlockSpec((None, ATT_BQ, GROUP_WIDTH),
                            lambda b, r, i: (b, jnp.maximum(i - 1, 0), r * ncb + cb))

    gspec = pl.BlockSpec((1, HEAD_DIM), lambda b, r, i: (0, 0))
    in_specs = [cur(cbs[0]), cur(cbs[1]), cur(cbs[2])]
    args = [pf, pf, pf]
    if has_prev:
        in_specs += [prev(cbs[1]), prev(cbs[2])]
        args += [pf, pf]
    in_specs += [gspec, gspec]
    args += [qg.reshape(1, HEAD_DIM), kg.reshape(1, HEAD_DIM)]
    o, lse = pl.pallas_call(
        functools.partial(_att_kernel, has_prev=has_prev),
        grid=(B, d, nq),
        in_specs=in_specs,
        out_specs=[pl.BlockSpec((None, ATT_BQ, GROUP_WIDTH), lambda b, r, i: (b, i, r)),
                   pl.BlockSpec((None, ATT_BQ, LANES), lambda b, r, i: (b, i, r))],
        out_shape=[jax.ShapeDtypeStruct((B, L, d * GROUP_WIDTH), F32),
                   jax.ShapeDtypeStruct((B, L, d * LANES), F32)],
        name=f"dilated_att_g{g}",
    )(*args)
    return o.reshape(B * S, GROUP_WIDTH), lse.reshape(B * S, LANES)


def _mix_kernel(*refs, moe):
    (al_ref, o0_ref, o1_ref, o2_ref, l0_ref, l1_ref, l2_ref, gl_ref, ga_ref, x_ref,
     wl_ref, wa_ref, wo_ref, nrm_ref) = refs[:14]
    if moe:
        rw_ref, rb_ref, xo_ref, h_ref, lg_ref, att_s, acc_s = refs[14:]
    else:
        xo_ref, h_ref, att_s, acc_s = refs[14:]
    n = pl.program_id(1)

    @pl.when(n == 0)
    def _():
        for h in range(GROUP_HEADS):
            sl = slice(h * HEAD_DIM, (h + 1) * HEAD_DIM)
            ls = [l[:, h * LSE_LANES:h * LSE_LANES + 1] for l in (l0_ref, l1_ref, l2_ref)]
            mx = jnp.maximum(jnp.maximum(ls[0], ls[1]), ls[2])
            ws = [jnp.exp(l - mx) for l in ls]
            o = (ws[0] * o0_ref[:, sl] + ws[1] * o1_ref[:, sl] + ws[2] * o2_ref[:, sl]) / (ws[0] + ws[1] + ws[2])
            att_s[:, sl] = o.astype(att_s.dtype)
        acc_s[...] = jnp.zeros_like(acc_s)

    y_lru = _dot(al_ref[...], wl_ref[...])
    y_att = _dot(att_s[...], wa_ref[...])
    merged = (jax.nn.sigmoid(gl_ref[...].astype(F32)) * y_lru
              + jax.nn.sigmoid(ga_ref[...].astype(F32)) * y_att)
    acc_s[...] += _dot(merged.astype(BF16), wo_ref[...])

    @pl.when(n == pl.num_programs(1) - 1)
    def _():
        xn = x_ref[...] + acc_s[...]
        xo_ref[...] = xn
        hn = _rms(xn, nrm_ref[...])
        h_ref[...] = hn.astype(h_ref.dtype)
        if moe:
            lg_ref[...] = jnp.dot(hn, rw_ref[...], precision=lax.Precision.HIGHEST,
                                  preferred_element_type=F32) + rb_ref[...]


def _mix_out(a_lru, outs, lses, proj, x, w_lru_out, w_att_out, w_out, ffn_gain, router=None):
    T, D = x.shape
    moe = router is not None
    nn = D // MIX_TN
    row = lambda w: pl.BlockSpec((MIX_TM, w), lambda i, n: (i, 0))
    in_specs = [row(LRU_WIDTH)] + [row(GROUP_WIDTH)] * 3 + [row(LANES)] * 3 + [
        pl.BlockSpec((MIX_TM, MIX_TN), lambda i, n: (i, CB_G_LRU + n)),
        pl.BlockSpec((MIX_TM, MIX_TN), lambda i, n: (i, CB_G_ATT + n)),
        row(D),
        pl.BlockSpec((LRU_WIDTH, MIX_TN), lambda i, n: (0, n)),
        pl.BlockSpec((GROUP_WIDTH, MIX_TN), lambda i, n: (0, n)),
        pl.BlockSpec((MIX_TN, D), lambda i, n: (n, 0)),
        pl.BlockSpec((1, D), lambda i, n: (0, 0)),
    ]
    args = [a_lru, *outs, *lses, proj, proj, x, w_lru_out, w_att_out, w_out, ffn_gain.reshape(1, D)]
    out_specs = [row(D), row(D)]
    out_shape = [jax.ShapeDtypeStruct((T, D), F32), jax.ShapeDtypeStruct((T, D), F32 if moe else BF16)]
    if moe:
        rw, rb = router
        in_specs += [pl.BlockSpec((D, LANES), lambda i, n: (0, 0)), pl.BlockSpec((1, LANES), lambda i, n: (0, 0))]
        args += [rw, rb]
        out_specs.append(row(LANES))
        out_shape.append(jax.ShapeDtypeStruct((T, LANES), F32))
    return pl.pallas_call(
        functools.partial(_mix_kernel, moe=moe),
        grid=(T // MIX_TM, nn),
        in_specs=in_specs,
        out_specs=out_specs,
        out_shape=out_shape,
        scratch_shapes=[pltpu.VMEM((MIX_TM, GROUP_WIDTH), BF16), pltpu.VMEM((MIX_TM, D), F32)],
        compiler_params=pltpu.CompilerParams(dimension_semantics=("parallel", "arbitrary")),
        name="mix_out_moe" if moe else "mix_out",
    )(*args)


def _swiglu_partial(xb, w1, w3, w2):
    g = (jax.nn.silu(_dot(xb, w1)) * _dot(xb, w3)).astype(BF16)
    return _dot(g, w2)


def _ffn_kernel(*refs, has_next):
    if has_next:
        h_ref, x_ref, w1_ref, w3_ref, w2_ref, g_ref, xo_ref, hn_ref, acc_s = refs
    else:
        h_ref, x_ref, w1_ref, w3_ref, w2_ref, xo_ref, acc_s = refs
    f = pl.program_id(1)
    part = _swiglu_partial(h_ref[...], w1_ref[...], w3_ref[...], w2_ref[...])

    @pl.when(f == 0)
    def _():
        acc_s[...] = x_ref[...] + part

    @pl.when(f > 0)
    def _():
        acc_s[...] += part

    @pl.when(f == pl.num_programs(1) - 1)
    def _():
        xn = acc_s[...]
        xo_ref[...] = xn
        if has_next:
            hn_ref[...] = _rms(xn, g_ref[...]).astype(hn_ref.dtype)


def _dense_ffn(h, x, w1, w3, w2, next_gain):
    T, D = x.shape
    FF = w1.shape[1]
    has_next = next_gain is not None
    row = pl.BlockSpec((FFN_TM, D), lambda i, f: (i, 0))
    in_specs = [row, row,
                pl.BlockSpec((D, FFN_TF), lambda i, f: (0, f)),
                pl.BlockSpec((D, FFN_TF), lambda i, f: (0, f)),
                pl.BlockSpec((FFN_TF, D), lambda i, f: (f, 0))]
    args = [h, x, w1, w3, w2]
    out_specs, out_shape = [row], [jax.ShapeDtypeStruct((T, D), F32)]
    if has_next:
        in_specs.append(pl.BlockSpec((1, D), lambda i, f: (0, 0)))
        args.append(next_gain.reshape(1, D))
        out_specs.append(row)
        out_shape.append(jax.ShapeDtypeStruct((T, D), BF16))
    res = pl.pallas_call(
        functools.partial(_ffn_kernel, has_next=has_next),
        grid=(T // FFN_TM, FF // FFN_TF),
        in_specs=in_specs, out_specs=out_specs, out_shape=out_shape,
        scratch_shapes=[pltpu.VMEM((FFN_TM, D), F32)],
        compiler_params=pltpu.CompilerParams(dimension_semantics=("parallel", "arbitrary")),
        name="dense_ffn",
    )(*args)
    return (res[0], res[1]) if has_next else (res[0], None)


def _router_kernel(lg_ref, idx_ref, wt_ref, cnt_ref, carry_s):
    tm = lg_ref.shape[0]

    @pl.when(pl.program_id(0) == 0)
    def _():
        carry_s[...] = jnp.zeros_like(carry_s)

    lane = lax.broadcasted_iota(jnp.int32, (tm, LANES), 1)
    lg = jnp.where(lane < N_EXPERTS, lg_ref[...], -jnp.inf)
    m1 = jnp.max(lg, axis=-1, keepdims=True)
    i1 = jnp.min(jnp.where(lg == m1, lane, LANES), axis=-1, keepdims=True)
    lg2 = jnp.where(lane == i1, -jnp.inf, lg)
    m2 = jnp.max(lg2, axis=-1, keepdims=True)
    i2 = jnp.min(jnp.where(lg2 == m2, lane, LANES), axis=-1, keepdims=True)
    e = jnp.exp(m2 - m1)
    w1 = 1.0 / (1.0 + e)
    w2 = e / (1.0 + e)
    hit = (lane == i1) | (lane == i2)
    rr = lax.broadcasted_iota(jnp.int32, (tm, tm), 0)
    cc = lax.broadcasted_iota(jnp.int32, (tm, tm), 1)
    excl = _dot((rr > cc).astype(BF16), hit.astype(BF16)) + carry_s[...]
    r1 = jnp.sum(jnp.where(lane == i1, excl, 0.0), axis=-1, keepdims=True).astype(jnp.int32)
    r2 = jnp.sum(jnp.where(lane == i2, excl, 0.0), axis=-1, keepdims=True).astype(jnp.int32)
    idx_ref[...] = jnp.where(lane == 0, i1, jnp.where(lane == 1, i2, jnp.where(lane == 2, r1, jnp.where(lane == 3, r2, 0))))
    wt_ref[...] = jnp.where(lane == 0, w1, jnp.where(lane == 1, w2, 0.0))
    carry_s[...] += jnp.sum(hit.astype(F32), axis=0, keepdims=True)
    cnt_ref[...] = carry_s[...].astype(jnp.int32)


def _router(logits):
    T = logits.shape[0]
    row = pl.BlockSpec((ROUTER_TM, LANES), lambda i: (i, 0))
    return pl.pallas_call(
        _router_kernel,
        grid=(T // ROUTER_TM,),
        in_specs=[row],
        out_specs=[row, row, pl.BlockSpec((1, LANES), lambda i: (0, 0))],
        out_shape=[jax.ShapeDtypeStruct((T, LANES), jnp.int32),
                   jax.ShapeDtypeStruct((T, LANES), F32),
                   jax.ShapeDtypeStruct((1, LANES), jnp.int32)],
        scratch_shapes=[pltpu.VMEM((1, LANES), F32)],
        compiler_params=pltpu.CompilerParams(dimension_semantics=("arbitrary",)),
        name="moe_router",
    )(logits)


def _row_copy(src, s, dst, d, sem):
    return pltpu.make_async_copy(src.at[pl.ds(s, 1)], dst.at[pl.ds(d, 1)], sem)


def _dispatch_kernel(src_ref, h_hbm, xs_ref, buf, sem):
    tm = xs_ref.shape[0]
    base = pl.program_id(0) * tm

    def issue(j, carry):
        _row_copy(h_hbm, src_ref[base + j], buf, j, sem).start()
        return carry

    lax.fori_loop(0, tm, issue, 0, unroll=8)

    def drain(j, carry):
        _row_copy(h_hbm, 0, buf, 0, sem).wait()
        return carry

    lax.fori_loop(0, tm, drain, 0, unroll=8)
    xs_ref[...] = buf[...].astype(xs_ref.dtype)


def _dispatch(src, h):
    D = h.shape[1]
    n_rows = src.shape[0]
    return pl.pallas_call(
        _dispatch_kernel,
        grid_spec=pltpu.PrefetchScalarGridSpec(
            num_scalar_prefetch=1,
            grid=(n_rows // DISPATCH_TM,),
            in_specs=[pl.BlockSpec(memory_space=pl.ANY)],
            out_specs=pl.BlockSpec((DISPATCH_TM, D), lambda i, src: (i, 0)),
            scratch_shapes=[pltpu.VMEM((DISPATCH_TM, D), F32), pltpu.SemaphoreType.DMA(())],
        ),
        out_shape=jax.ShapeDtypeStruct((n_rows, D), BF16),
        compiler_params=pltpu.CompilerParams(dimension_semantics=("arbitrary",)),
        name="moe_dispatch",
    )(src, h)


def _expert_kernel(te_ref, na_ref, xs_ref, w1_ref, w3_ref, w2_ref, ys_ref, acc_s):
    del te_ref
    i = pl.program_id(0)
    f = pl.program_id(1)

    @pl.when(i < na_ref[0])
    def _():
        part = _swiglu_partial(xs_ref[...], w1_ref[0], w3_ref[0], w2_ref[0])

        @pl.when(f == 0)
        def _():
            acc_s[...] = part

        @pl.when(f > 0)
        def _():
            acc_s[...] += part

        @pl.when(f == pl.num_programs(1) - 1)
        def _():
            ys_ref[...] = acc_s[...]

    @pl.when((i >= na_ref[0]) & (f == pl.num_programs(1) - 1))
    def _():
        ys_ref[...] = jnp.zeros_like(ys_ref)


def _experts(tile_expert, n_active, xs, w1, w3, w2):
    P, D = xs.shape
    FF = w1.shape[2]
    nf = FF // MOE_TF

    def tile(i, na):
        return jnp.minimum(i, na[0] - 1)

    def fidx(i, f, na):
        return jnp.where(i < na[0], f, nf - 1)

    row = pl.BlockSpec((MOE_TM, D), lambda i, f, te, na: (tile(i, na), 0))
    return pl.pallas_call(
        _expert_kernel,
        grid_spec=pltpu.PrefetchScalarGridSpec(
            num_scalar_prefetch=2,
            grid=(P // MOE_TM, nf),
            in_specs=[row,
                      pl.BlockSpec((1, D, MOE_TF), lambda i, f, te, na: (te[tile(i, na)], 0, fidx(i, f, na))),
                      pl.BlockSpec((1, D, MOE_TF), lambda i, f, te, na: (te[tile(i, na)], 0, fidx(i, f, na))),
                      pl.BlockSpec((1, MOE_TF, D), lambda i, f, te, na: (te[tile(i, na)], fidx(i, f, na), 0))],
            out_specs=pl.BlockSpec((MOE_TM, D), lambda i, f, te, na: (i, 0)),
            scratch_shapes=[pltpu.VMEM((MOE_TM, D), F32)],
        ),
        out_shape=jax.ShapeDtypeStruct((P, D), F32),
        compiler_params=pltpu.CompilerParams(dimension_semantics=("arbitrary", "arbitrary")),
        name="moe_experts",
    )(tile_expert, n_active, xs, w1, w3, w2)


def _combine_kernel(*refs, has_next):
    if has_next:
        pos1_ref, pos2_ref, ys_hbm, x_ref, wt_ref, g_ref, xo_ref, hn_ref, b1, b2, sem = refs
    else:
        pos1_ref, pos2_ref, ys_hbm, x_ref, wt_ref, xo_ref, b1, b2, sem = refs
    tm = x_ref.shape[0]
    base = pl.program_id(0) * tm

    def issue(j, carry):
        t = base + j
        _row_copy(ys_hbm, pos1_ref[t], b1, j, sem).start()
        _row_copy(ys_hbm, pos2_ref[t], b2, j, sem).start()
        return carry

    lax.fori_loop(0, tm, issue, 0, unroll=8)

    def drain(j, carry):
        _row_copy(ys_hbm, 0, b1, 0, sem).wait()
        _row_copy(ys_hbm, 0, b2, 0, sem).wait()
        return carry

    lax.fori_loop(0, tm, drain, 0, unroll=8)
    xn = x_ref[...] + (wt_ref[:, 0:1] * b1[...] + wt_ref[:, 1:2] * b2[...])
    xo_ref[...] = xn
    if has_next:
        hn_ref[...] = _rms(xn, g_ref[...]).astype(hn_ref.dtype)


def _combine(pos1, pos2, ys, x, wts, next_gain):
    T, D = x.shape
    has_next = next_gain is not None
    row = pl.BlockSpec((COMBINE_TM, D), lambda i, p1, p2: (i, 0))
    in_specs = [pl.BlockSpec(memory_space=pl.ANY), row,
                pl.BlockSpec((COMBINE_TM, LANES), lambda i, p1, p2: (i, 0))]
    args = [ys, x, wts]
    out_specs, out_shape = [row], [jax.ShapeDtypeStruct((T, D), F32)]
    if has_next:
        in_specs.append(pl.BlockSpec((1, D), lambda i, p1, p2: (0, 0)))
        args.append(next_gain.reshape(1, D))
        out_specs.append(row)
        out_shape.append(jax.ShapeDtypeStruct((T, D), BF16))
    res = pl.pallas_call(
        functools.partial(_combine_kernel, has_next=has_next),
        grid_spec=pltpu.PrefetchScalarGridSpec(
            num_scalar_prefetch=2,
            grid=(T // COMBINE_TM,),
            in_specs=in_specs, out_specs=out_specs,
            scratch_shapes=[pltpu.VMEM((COMBINE_TM, D), F32), pltpu.VMEM((COMBINE_TM, D), F32),
                            pltpu.SemaphoreType.DMA(())],
        ),
        out_shape=out_shape,
        compiler_params=pltpu.CompilerParams(dimension_semantics=("arbitrary",)),
        name="moe_combine",
    )(pos1, pos2, *args)
    return (res[0], res[1]) if has_next else (res[0], None)


def _moe_ffn(h, logits, x, w1, w3, w2, next_gain):
    T, D = x.shape
    idx, wts, cnt = _router(logits)
    e1, e2, r1, r2 = idx[:, 0], idx[:, 1], idx[:, 2], idx[:, 3]
    counts = cnt[0, :N_EXPERTS]
    padded = (counts + MOE_TM - 1) // MOE_TM * MOE_TM
    ends = jnp.cumsum(padded)
    starts = ends - padded
    experts = jnp.arange(N_EXPERTS, dtype=jnp.int32)
    pos1 = jnp.sum(jnp.where(e1[:, None] == experts, starts, 0), axis=1).astype(jnp.int32) + r1
    pos2 = jnp.sum(jnp.where(e2[:, None] == experts, starts, 0), axis=1).astype(jnp.int32) + r2
    n_tiles = 2 * T // MOE_TM + N_EXPERTS
    n_active = (ends[-1] // MOE_TM).astype(jnp.int32).reshape(1)
    tile_rows = jnp.arange(n_tiles, dtype=jnp.int32) * MOE_TM
    tile_expert = jnp.minimum(jnp.sum(tile_rows[:, None] >= ends[None, :], axis=1), N_EXPERTS - 1).astype(jnp.int32)
    tokens = jnp.arange(T, dtype=jnp.int32)
    src = jnp.zeros((n_tiles * MOE_TM,), jnp.int32).at[pos1].set(tokens).at[pos2].set(tokens)
    xs = _dispatch(src, h)
    ys = _experts(tile_expert, n_active, xs, w1, w3, w2)
    return _combine(pos1, pos2, ys, x, wts, next_gain)


def kernel(x, mix_norm, w_in, conv_w, conv_b, lru_wa, lru_ba, lru_wx, lru_bx, lru_lambda, q_norm, k_norm,
           w_lru_out, w_att_out, w_out, ffn_norm, dense_w1, dense_w3, dense_w2, router_w, router_b,
           moe_w1, moe_w3, moe_w2):
    B, S, D = x.shape
    T = B * S
    depth = w_in.shape[0]
    cast = lambda w: w.astype(BF16)
    xf = x.reshape(T, D)
    h = _rmsnorm(xf, mix_norm[0])
    for l in range(depth):
        proj = _in_proj(h, cast(w_in[l]))
        proj3 = proj.reshape(B, S, IN_COLS)
        a_lru = _lru_branch(proj3, conv_w[l], conv_b[l], cast(lru_wa[l]), lru_ba[l], cast(lru_wx[l]),
                            lru_bx[l], lru_lambda[l]).reshape(T, LRU_WIDTH)
        groups = [_att_group(proj3, g, q_norm[l], k_norm[l]) for g in range(len(ATT_DILATIONS))]
        outs = [o for o, _ in groups]
        lses = [s for _, s in groups]
        next_gain = mix_norm[l + 1] if l + 1 < depth else None
        j = l // 2
        if l % 2 == 0:
            xf, h2 = _mix_out(a_lru, outs, lses, proj, xf, cast(w_lru_out[l]), cast(w_att_out[l]),
                              cast(w_out[l]), ffn_norm[l])
            xf, h = _dense_ffn(h2, xf, cast(dense_w1[j]), cast(dense_w3[j]), cast(dense_w2[j]), next_gain)
        else:
            rw = jnp.pad(router_w[j], ((0, 0), (0, LANES - N_EXPERTS)))
            rb = jnp.pad(router_b[j], (0, LANES - N_EXPERTS)).reshape(1, LANES)
            xf, h2, logits = _mix_out(a_lru, outs, lses, proj, xf, cast(w_lru_out[l]), cast(w_att_out[l]),
                                      cast(w_out[l]), ffn_norm[l], router=(rw, rb))
            xf, h = _moe_ffn(h2, logits, xf, cast(moe_w1[j]), cast(moe_w3[j]), cast(moe_w2[j]), next_gain)
    return xf.reshape(B, S, D)
```
